```python
import math
import jax, jax.numpy as jnp
from jax import lax
import numpy as np

D_MODEL = 4096
BATCH = 1
SEQ = 16384
DEPTH = 4

D_MIX = D_MODEL
D_S5 = D_MIX // 4
D_HG = D_MIX // 4
D_GLA = D_MIX // 4
D_RW = D_MIX - D_S5 - D_HG - D_GLA

S5_CH = 16
S5_STATE = 64
S5_GROUPS = D_S5 // S5_CH

HG_DK = 128
HG_HEADS = D_HG // HG_DK
HG_DV = D_HG // HG_HEADS

GLA_HEADS = 4
GLA_QK = D_GLA // 2
GLA_DK = GLA_QK // GLA_HEADS
GLA_DV = D_GLA // GLA_HEADS
GLA_LORA = 16
GLA_TAU = 16.0

RW_HEAD = 64
RW_HEADS = D_RW // RW_HEAD
RW_LORA_W = max(32, int(round(1.8 * D_RW ** 0.5 / 32)) * 32)
RW_LORA_A = max(32, int(round(1.8 * D_RW ** 0.5 / 32)) * 32)
RW_LORA_MV = max(32, int(round(1.3 * D_RW ** 0.5 / 32)) * 32)
RW_LORA_G = max(32, int(round(0.6 * D_RW ** 0.8 / 32)) * 32)

CHUNK = 64

BASE_SIZES = (D_S5, D_HG, D_HG, D_HG, D_HG, GLA_QK, GLA_QK, D_GLA, D_GLA, GLA_LORA)
RW_SIZES = (D_RW, D_RW, D_RW, RW_LORA_W, RW_LORA_A, RW_LORA_G)
BASE_COLS = sum(BASE_SIZES)
RW_COLS = sum(RW_SIZES)
PROJ_COLS = BASE_COLS + RW_COLS

N_EXPERTS = 64
TOP_K = 8
N_EXPERT_GROUPS = 8
TOPK_GROUPS = 4
D_EXPERT = 128
D_SHARED = 512
ROUTED_SCALE = 2.5

DEEPNORM_ALPHA = (2.0 * DEPTH) ** 0.25
DEEPNORM_BETA = (8.0 * DEPTH) ** -0.25

LN_EPS = 1e-5
RMS_EPS = 1e-6
RW_GN_EPS = 64e-5

kernel_name = "hymba_s5_hgrn2_gla_rwkv7_moe_deepnorm"


def _layer_norm(x, g, b):
    x = x.astype(jnp.float32)
    mu = jnp.mean(x, -1, keepdims=True)
    var = jnp.mean(jnp.square(x - mu), -1, keepdims=True)
    return (x - mu) * lax.rsqrt(var + LN_EPS) * g + b


def _head_rms_norm(x, g, n_heads):
    b, t, c = x.shape
    xh = x.astype(jnp.float32).reshape(b, t, n_heads, c // n_heads)
    xh = xh * lax.rsqrt(jnp.mean(jnp.square(xh), -1, keepdims=True) + RMS_EPS)
    return xh.reshape(b, t, c) * g


def _head_group_norm(x, g, bias, n_heads, eps):
    b, t, c = x.shape
    xh = x.astype(jnp.float32).reshape(b, t, n_heads, c // n_heads)
    mu = jnp.mean(xh, -1, keepdims=True)
    var = jnp.mean(jnp.square(xh - mu), -1, keepdims=True)
    return ((xh - mu) * lax.rsqrt(var + eps)).reshape(b, t, c) * g + bias


def _token_shift(y):
    return jnp.pad(y, ((0, 0), (1, 0), (0, 0)))[:, :-1]


def _split(x, sizes):
    out, start = [], 0
    for s in sizes:
        out.append(x[..., start:start + s])
        start += s
    return out


def _heads(x, n):
    b, t, c = x.shape
    return x.reshape(b, t, n, c // n).transpose(0, 2, 1, 3)


def _merge(x):
    b, h, t, d = x.shape
    return x.transpose(0, 2, 1, 3).reshape(b, t, h * d)


def _chunked_gated_linear_attention(q, k, v, log_g):
    bt, h, t, kd = q.shape
    vd = v.shape[-1]
    n = t // CHUNK

    def to_chunks(a):
        return a.astype(jnp.float32).reshape(bt, h, n, CHUNK, a.shape[-1]).transpose(2, 0, 1, 3, 4)

    qc, kc, vc, gc = (to_chunks(a) for a in (q, k, v, log_g))
    bc = jnp.cumsum(gc, axis=-2)
    causal = jnp.tril(jnp.ones((CHUNK, CHUNK), bool))[:, :, None]

    def step(state, inp):
        q_n, k_n, v_n, b_n = inp
        diff = b_n[:, :, :, None, :] - b_n[:, :, None, :, :]
        decay = jnp.where(causal, jnp.exp(jnp.where(causal, diff, 0.0)), 0.0)
        scores = jnp.einsum('bhik,bhjk,bhijk->bhij', q_n, k_n, decay)
        o = (jnp.einsum('bhij,bhjv->bhiv', scores, v_n)
             + jnp.einsum('bhik,bhkv->bhiv', q_n * jnp.exp(b_n), state))
        b_last = b_n[:, :, -1, :]
        state = (jnp.exp(b_last)[..., None] * state
                 + jnp.einsum('bhjk,bhjv->bhkv', k_n * jnp.exp(b_last[:, :, None, :] - b_n), v_n))
        return state, o

    s0 = jnp.zeros((bt, h, kd, vd), jnp.float32)
    _, o = lax.scan(step, s0, (qc, kc, vc, bc))
    return o.transpose(1, 2, 0, 3, 4).reshape(bt, h, t, vd)


def _s5_mixer(u, a_re, a_im, log_dt, b_re, b_im, c_re, c_im, d, w_glu, b_glu, norm_g):
    bt, t, _ = u.shape
    ug = u.astype(jnp.float32).reshape(bt, t, S5_GROUPS, S5_CH)
    lam = lax.complex(jnp.minimum(a_re.astype(jnp.float32), -1e-4), a_im.astype(jnp.float32))
    dt = jnp.exp(log_dt.astype(jnp.float32))[:, None]
    a_bar = jnp.exp(lam * dt)
    b_mat = lax.complex(b_re.astype(jnp.float32), b_im.astype(jnp.float32))
    b_bar = ((a_bar - 1.0) / lam)[..., None] * b_mat
    bu = jnp.einsum('gpc,btgc->btgp', b_bar, ug.astype(jnp.complex64))
    a_seq = jnp.broadcast_to(a_bar, bu.shape)

    def combine(left, right):
        a_l, x_l = left
        a_r, x_r = right
        return a_l * a_r, a_r * x_l + x_r

    _, states = lax.associative_scan(combine, (a_seq, bu), axis=1)
    c_mat = lax.complex(c_re.astype(jnp.float32), c_im.astype(jnp.float32))
    y = jnp.real(jnp.einsum('gcp,btgp->btgc', c_mat, states)) + d.reshape(S5_GROUPS, S5_CH) * ug
    y = jax.nn.gelu(y.reshape(bt, t, D_S5))
    y = y * jax.nn.sigmoid(y @ w_glu + b_glu)
    return _head_rms_norm(y, norm_g, 1)


def _hgrn2_mixer(q, f, i, g, lb, norm_g):
    f = f.astype(jnp.float32)
    log_f = jnp.logaddexp(jnp.log(lb), jnp.log1p(-lb) + jax.nn.log_sigmoid(f))
    k = (1.0 - lb) * jax.nn.sigmoid(-f)
    o = _chunked_gated_linear_attention(_heads(jax.nn.silu(q), HG_HEADS), _heads(k, HG_HEADS),
                                        _heads(i, HG_HEADS), _heads(log_f, HG_HEADS))
    return _head_rms_norm(_merge(o), norm_g, HG_HEADS) * jax.nn.silu(g)


def _gla_mixer(q, k, v, g, a_lr, w_a2, b_a, norm_g):
    log_alpha = jax.nn.log_sigmoid((a_lr @ w_a2 + b_a).astype(jnp.float32)) / GLA_TAU
    o = _chunked_gated_linear_attention(_heads(q * GLA_DK ** -0.5, GLA_HEADS), _heads(k, GLA_HEADS),
                                        _heads(v, GLA_HEADS), _heads(log_alpha, GLA_HEADS))
    return _head_rms_norm(_merge(o), norm_g, GLA_HEADS) * jax.nn.silu(g)


def _rwkv7_scan(r, w, k, v, a, b):
    bt, t, _ = r.shape

    def to_seq(x):
        return x.astype(jnp.float32).reshape(bt, t, RW_HEADS, RW_HEAD).transpose(1, 0, 2, 3)

    def step(state, inp):
        r_t, w_t, k_t, v_t, a_t, b_t = inp
        sa = jnp.einsum('bhij,bhj->bhi', state, a_t)
        state = (state * w_t[:, :, None, :] + sa[..., None] * b_t[:, :, None, :]
                 + v_t[..., None] * k_t[:, :, None, :])
        return state, jnp.einsum('bhij,bhj->bhi', state, r_t)

    s0 = jnp.zeros((bt, RW_HEADS, RW_HEAD, RW_HEAD), jnp.float32)
    _, y = lax.scan(step, s0, (to_seq(r), to_seq(w), to_seq(k), to_seq(v), to_seq(a), to_seq(b)))
    return y.transpose(1, 0, 2, 3).reshape(bt, t, D_RW)


def _rwkv7_mixer(r, k, v, xw, xa, xg, w0, w2, a0, a2, g2, k_k, k_a, r_k, ln_g, ln_b):
    bt, t, c = r.shape
    w_log = -jax.nn.softplus(-(w0 + jnp.tanh(xw) @ w2)) - 0.5
    decay = jnp.exp(-jnp.exp(w_log.astype(jnp.float32)))
    a = jax.nn.sigmoid(a0 + xa @ a2)
    gate = jax.nn.sigmoid(xg) @ g2
    kk = (k * k_k).reshape(bt, t, RW_HEADS, RW_HEAD)
    kk = kk / jnp.maximum(jnp.linalg.norm(kk, axis=-1, keepdims=True), 1e-12)
    kk = kk.reshape(bt, t, c)
    k = k * (1.0 + (a - 1.0) * k_a)
    y = _rwkv7_scan(r, decay, k, v, -kk, kk * a)
    y = _head_group_norm(y, ln_g, ln_b, RW_HEADS, RW_GN_EPS)
    bonus = (jnp.sum((r * k * r_k).reshape(bt, t, RW_HEADS, RW_HEAD), -1, keepdims=True)
             * v.reshape(bt, t, RW_HEADS, RW_HEAD))
    return (y + bonus.reshape(bt, t, c)) * gate


def _moe(x, router, bias, w_gate, w_up, w_down, sw_gate, sw_up, sw_down):
    bt, t, d = x.shape
    xt = x.reshape(bt * t, d)
    scores = jax.nn.sigmoid((xt @ router).astype(jnp.float32))
    sel = scores + bias
    per_group = N_EXPERTS // N_EXPERT_GROUPS
    grp_score = lax.top_k(sel.reshape(-1, N_EXPERT_GROUPS, per_group), 2)[0].sum(-1)
    _, top_grp = lax.top_k(grp_score, TOPK_GROUPS)
    grp_mask = jax.nn.one_hot(top_grp, N_EXPERT_GROUPS, dtype=jnp.float32).sum(1)
    expert_mask = jnp.repeat(grp_mask, per_group, axis=-1) > 0
    _, idx = lax.top_k(jnp.where(expert_mask, sel, -jnp.inf), TOP_K)
    wts = jnp.take_along_axis(scores, idx, -1)
    wts = wts / jnp.sum(wts, -1, keepdims=True) * ROUTED_SCALE
    combine = jnp.sum(jax.nn.one_hot(idx, N_EXPERTS, dtype=jnp.float32) * wts[..., None], 1)
    hid = jax.nn.silu(jnp.einsum('nd,edf->nef', xt, w_gate)) * jnp.einsum('nd,edf->nef', xt, w_up)
    routed = jnp.einsum('nef,efd->nd', hid * combine[..., None], w_down)
    shared = (jax.nn.silu(xt @ sw_gate) * (xt @ sw_up)) @ sw_down
    return (routed + shared).reshape(bt, t, d)


def setup_inputs(seed: int = 0) -> dict:
    key = jax.random.key(seed)
    keys = iter(jax.random.split(key, 64))
    f32 = jnp.float32
    L = DEPTH

    def nrm(shape, scale):
        return scale * jax.random.normal(next(keys), shape, f32)

    def unif(shape, lo, hi):
        return jax.random.uniform(next(keys), shape, f32, lo, hi)

    n_idx = jnp.arange(S5_STATE, dtype=f32)
    decay_speed = -6.0 + 5.0 * (jnp.arange(D_RW, dtype=f32) / (D_RW - 1)) ** 0.85
    return {
        "x": nrm((BATCH, SEQ, D_MODEL), 1.0),
        "ln_in_g": 1.0 + nrm((D_MODEL,), 0.02),
        "ln_in_b": nrm((D_MODEL,), 0.02),
        "w_in": nrm((L, D_MODEL, PROJ_COLS), D_MODEL ** -0.5),
        "s5_a_re": -0.5 + nrm((L, S5_GROUPS, S5_STATE), 0.01),
        "s5_a_im": math.pi * n_idx + nrm((L, S5_GROUPS, S5_STATE), 0.01),
        "s5_log_dt": unif((L, S5_GROUPS), math.log(1e-3), math.log(1e-1)),
        "s5_b_re": nrm((L, S5_GROUPS, S5_STATE, S5_CH), (2 * S5_CH) ** -0.5),
        "s5_b_im": nrm((L, S5_GROUPS, S5_STATE, S5_CH), (2 * S5_CH) ** -0.5),
        "s5_c_re": nrm((L, S5_GROUPS, S5_CH, S5_STATE), S5_STATE ** -0.5),
        "s5_c_im": nrm((L, S5_GROUPS, S5_CH, S5_STATE), S5_STATE ** -0.5),
        "s5_d": nrm((L, D_S5), 0.5),
        "s5_w_glu": nrm((L, D_S5, D_S5), D_S5 ** -0.5),
        "s5_b_glu": nrm((L, D_S5), 0.02),
        "s5_norm_g": 1.0 + nrm((L, D_S5), 0.02),
        "hg_lb": nrm((L, D_HG), 0.1),
        "hg_norm_g": 1.0 + nrm((L, D_HG), 0.02),
        "gla_w_a2": nrm((L, GLA_LORA, GLA_QK), GLA_LORA ** -0.5),
        "gla_b_a": nrm((L, GLA_QK), 0.1),
        "gla_norm_g": 1.0 + nrm((L, D_GLA), 0.02),
        "rw_mu": unif((L, RW_COLS), 0.1, 0.9),
        "rw_w0": decay_speed + 0.5 + nrm((L, D_RW), 0.1),
        "rw_w2": nrm((L, RW_LORA_W, D_RW), 0.1 * RW_LORA_W ** -0.5),
        "rw_a0": nrm((L, D_RW), 0.1),
        "rw_a2": nrm((L, RW_LORA_A, D_RW), 0.1 * RW_LORA_A ** -0.5),
        "rw_g2": nrm((L, RW_LORA_G, D_RW), RW_LORA_G ** -0.5),
        "rw_v0": 1.0 + nrm((L - 1, D_RW), 0.1),
        "rw_v1": nrm((L - 1, D_RW, RW_LORA_MV), D_RW ** -0.5),
        "rw_v2": nrm((L - 1, RW_LORA_MV, D_RW), 0.1 * RW_LORA_MV ** -0.5),
        "rw_k_k": 0.85 + nrm((L, D_RW), 0.02),
        "rw_k_a": 1.0 + nrm((L, D_RW), 0.02),
        "rw_r_k": nrm((L, D_RW), 0.1),
        "rw_ln_g": 1.0 + nrm((L, D_RW), 0.02),
        "rw_ln_b": nrm((L, D_RW), 0.02),
        "w_out": nrm((L, D_MIX, D_MODEL), D_MIX ** -0.5 * DEEPNORM_BETA),
        "ln1_g": 1.0 + nrm((L, D_MODEL), 0.02),
        "ln1_b": nrm((L, D_MODEL), 0.02),
        "moe_router": nrm((L, D_MODEL, N_EXPERTS), D_MODEL ** -0.5),
        "moe_bias": nrm((L, N_EXPERTS), 0.01),
        "moe_w_gate": nrm((L, N_EXPERTS, D_MODEL, D_EXPERT), D_MODEL ** -0.5),
        "moe_w_up": nrm((L, N_EXPERTS, D_MODEL, D_EXPERT), D_MODEL ** -0.5),
        "moe_w_down": nrm((L, N_EXPERTS, D_EXPERT, D_MODEL), D_EXPERT ** -0.5 * DEEPNORM_BETA),
        "sh_w_gate": nrm((L, D_MODEL, D_SHARED), D_MODEL ** -0.5),
        "sh_w_up": nrm((L, D_MODEL, D_SHARED), D_MODEL ** -0.5),
        "sh_w_down": nrm((L, D_SHARED, D_MODEL), D_SHARED ** -0.5 * DEEPNORM_BETA),
        "ln2_g": 1.0 + nrm((L, D_MODEL), 0.02),
        "ln2_b": nrm((L, D_MODEL), 0.02),
    }


def reference(x, ln_in_g, ln_in_b, w_in, s5_a_re, s5_a_im, s5_log_dt, s5_b_re, s5_b_im, s5_c_re,
              s5_c_im, s5_d, s5_w_glu, s5_b_glu, s5_norm_g, hg_lb, hg_norm_g, gla_w_a2, gla_b_a,
              gla_norm_g, rw_mu, rw_w0, rw_w2, rw_a0, rw_a2, rw_g2, rw_v0, rw_v1, rw_v2, rw_k_k,
              rw_k_a, rw_r_k, rw_ln_g, rw_ln_b, w_out, ln1_g, ln1_b, moe_router, moe_bias,
              moe_w_gate, moe_w_up, moe_w_down, sh_w_gate, sh_w_up, sh_w_down, ln2_g, ln2_b):
    lb_all = jnp.cumsum(jax.nn.softmax(hg_lb.astype(jnp.float32), axis=0), axis=0)
    lb_all = lb_all - lb_all[0]
    h = _layer_norm(x, ln_in_g, ln_in_b)
    v_first = None
    for l in range(DEPTH):
        proj = h @ w_in[l]
        base, rw = proj[..., :BASE_COLS], proj[..., BASE_COLS:]
        rw = rw + rw_mu[l] * (_token_shift(rw) - rw)
        s5_u, hg_q, hg_f, hg_i, hg_g, gla_q, gla_k, gla_v, gla_g, gla_a = _split(base, BASE_SIZES)
        r_r, r_k, r_v, r_xw, r_xa, r_xg = _split(rw, RW_SIZES)
        if l == 0:
            v_first = r_v
        else:
            r_v = r_v + (v_first - r_v) * jax.nn.sigmoid(rw_v0[l - 1] + (r_v @ rw_v1[l - 1]) @ rw_v2[l - 1])

        y_s5 = _s5_mixer(s5_u, s5_a_re[l], s5_a_im[l], s5_log_dt[l], s5_b_re[l], s5_b_im[l],
                         s5_c_re[l], s5_c_im[l], s5_d[l], s5_w_glu[l], s5_b_glu[l], s5_norm_g[l])
        y_hg = _hgrn2_mixer(hg_q, hg_f, hg_i, hg_g, lb_all[l], hg_norm_g[l])
        y_gla = _gla_mixer(gla_q, gla_k, gla_v, gla_g, gla_a, gla_w_a2[l], gla_b_a[l], gla_norm_g[l])
        y_rw = _rwkv7_mixer(r_r, r_k, r_v, r_xw, r_xa, r_xg, rw_w0[l], rw_w2[l], rw_a0[l], rw_a2[l],
                            rw_g2[l], rw_k_k[l], rw_k_a[l], rw_r_k[l], rw_ln_g[l], rw_ln_b[l])

        mix = jnp.concatenate([y_s5, y_hg, y_gla, y_rw], axis=-1) @ w_out[l]
        h = _layer_norm(DEEPNORM_ALPHA * h + mix, ln1_g[l], ln1_b[l])
        ffn = _moe(h, moe_router[l], moe_bias[l], moe_w_gate[l], moe_w_up[l], moe_w_down[l],
                   sh_w_gate[l], sh_w_up[l], sh_w_down[l])
        h = _layer_norm(DEEPNORM_ALPHA * h + ffn, ln2_g[l], ln2_b[l])
    return h.astype(x.dtype)
```

```python
import functools
import math

import jax
import jax.numpy as jnp
from jax import lax
from jax.experimental import pallas as pl
from jax.experimental.pallas import tpu as pltpu

F32 = jnp.float32
BF16 = jnp.bfloat16

D_MODEL = 4096
DEPTH = 4
D_S5 = D_HG = D_GLA = D_RW = 1024
S5_CH, S5_STATE, S5_GROUPS = 16, 64, 64
HG_HEADS, HG_DK = 8, 128
GLA_HEADS, GLA_QK, GLA_DK, GLA_DV, GLA_LORA, GLA_TAU = 4, 512, 128, 256, 16, 16.0
RW_HEADS, RW_HEAD = 16, 64
RW_LORA_W, RW_LORA_A, RW_LORA_MV, RW_LORA_G = 64, 64, 32, 160
CHUNK = 64
N_EXPERTS, TOP_K, N_GROUPS, TOPK_GROUPS, D_EXPERT, D_SHARED = 64, 8, 8, 4, 128, 512
ROUTED_SCALE = 2.5
ALPHA = (2.0 * DEPTH) ** 0.25
LN_EPS, RMS_EPS, RW_GN_EPS = 1e-5, 1e-6, 64e-5

REF_BASE_COLS = 8208
C_S5, C_HGQ, C_HGF, C_HGI, C_HGG = 0, 1024, 2048, 3072, 4096
C_GQ, C_GK, C_GV, C_GG = 5120, 5632, 6144, 7168
C_RR, C_RK, C_RV = 8192, 9216, 10240
C_GA, C_XWA, C_XG = 11264, 11392, 11520
PROJ_W = 11776
MOE_HID = N_EXPERTS * D_EXPERT + D_SHARED
LANE = 128
VMEM_LIMIT = 56 * 1024 * 1024


def _cp(sem, vmem=VMEM_LIMIT):
    return pltpu.CompilerParams(dimension_semantics=sem, vmem_limit_bytes=vmem)


def _dot(a, b):
    return jnp.dot(a.astype(BF16), b.astype(BF16), preferred_element_type=F32)


def _dot_nt(a, b):
    return lax.dot_general(a.astype(BF16), b.astype(BF16), (((1,), (1,)), ((), ())),
                           preferred_element_type=F32)


def _dot_tn(a, b):
    return lax.dot_general(a.astype(BF16), b.astype(BF16), (((0,), (0,)), ((), ())),
                           preferred_element_type=F32)


def _split3(x):
    hi = x.astype(BF16)
    r1 = x - hi.astype(F32)
    mid = r1.astype(BF16)
    lo = (r1 - mid.astype(F32)).astype(BF16)
    return hi, mid, lo


def _tril(n, strict=False):
    r = lax.broadcasted_iota(jnp.int32, (n, n), 0)
    c = lax.broadcasted_iota(jnp.int32, (n, n), 1)
    return (r > c) if strict else (r >= c)


def _cumsum_rows(x):
    n = x.shape[0]
    tri = jnp.where(_tril(n), 1.0, 0.0).astype(BF16)
    hi, mid, lo = _split3(x)
    d = functools.partial(jnp.dot, preferred_element_type=F32)
    return d(tri, hi) + d(tri, mid) + d(tri, lo)


def _sigmoid(x):
    return 1.0 / (1.0 + jnp.exp(-x))


def _silu(x):
    return x * _sigmoid(x)


def _log_sigmoid(x):
    return jnp.minimum(x, 0.0) - jnp.log1p(jnp.exp(-jnp.abs(x)))


def _mm_kernel(x_ref, w_ref, o_ref):
    o_ref[...] = jnp.dot(x_ref[...], w_ref[...], preferred_element_type=F32).astype(o_ref.dtype)


def matmul(x, w, tm, tn, out_dtype):
    m, k = x.shape
    n = w.shape[1]
    tm, tn = min(tm, m), min(tn, n)
    return pl.pallas_call(
        _mm_kernel, grid=(m // tm, n // tn),
        in_specs=[pl.BlockSpec((tm, k), lambda i, j: (i, 0)), pl.BlockSpec((k, tn), lambda i, j: (0, j))],
        out_specs=pl.BlockSpec((tm, tn), lambda i, j: (i, j)),
        out_shape=jax.ShapeDtypeStruct((m, n), out_dtype),
        compiler_params=_cp(("parallel", "parallel")), name="matmul")(x, w)


def _ln_body(x, g_ref, b_ref, o32_ref, o16_ref):
    mu = jnp.mean(x, axis=-1, keepdims=True)
    xc = x - mu
    var = jnp.mean(xc * xc, axis=-1, keepdims=True)
    y = xc * lax.rsqrt(var + LN_EPS) * g_ref[...] + b_ref[...]
    o32_ref[...] = y
    o16_ref[...] = y.astype(BF16)


def _ln_kernel(x_ref, g_ref, b_ref, o32_ref, o16_ref):
    _ln_body(x_ref[...], g_ref, b_ref, o32_ref, o16_ref)


def _res_ln_kernel(h_ref, y_ref, g_ref, b_ref, o32_ref, o16_ref):
    _ln_body(ALPHA * h_ref[...] + y_ref[...], g_ref, b_ref, o32_ref, o16_ref)


def _ln_call(kern, arrays, g, b, tb=256):
    t, d = arrays[0].shape
    tb = min(tb, t)
    row = pl.BlockSpec((tb, d), lambda i: (i, 0))
    vec = pl.BlockSpec((1, d), lambda i: (0, 0))
    return pl.pallas_call(
        kern, grid=(t // tb,), in_specs=[row] * len(arrays) + [vec, vec], out_specs=[row, row],
        out_shape=[jax.ShapeDtypeStruct((t, d), F32), jax.ShapeDtypeStruct((t, d), BF16)],
        compiler_params=_cp(("parallel",)), name="layernorm")(*arrays, g.reshape(1, d), b.reshape(1, d))


def layer_norm(x, g, b):
    return _ln_call(_ln_kernel, [x], g, b)


def residual_layer_norm(h, y, g, b):
    return _ln_call(_res_ln_kernel, [h, y], g, b)


def s5_operators(a_re, a_im, log_dt, b_re, b_im, c_re, c_im):
    L, P, CH, G = CHUNK, S5_STATE, S5_CH, S5_GROUPS
    hp = lax.Precision.HIGHEST
    lr = jnp.minimum(a_re.astype(F32), -1e-4)
    li = a_im.astype(F32)
    dt = jnp.exp(log_dt.astype(F32))[:, None]
    zr, zi = lr * dt, li * dt
    er = jnp.exp(zr)
    abr, abi = er * jnp.cos(zi), er * jnp.sin(zi)
    den = lr * lr + li * li
    nr, ni = abr - 1.0, abi
    qr, qi = (nr * lr + ni * li) / den, (ni * lr - nr * li) / den
    bbr = qr[..., None] * b_re - qi[..., None] * b_im
    bbi = qr[..., None] * b_im + qi[..., None] * b_re
    tau = jnp.arange(L + 1, dtype=F32)[None, :, None]
    pe = jnp.exp(zr[:, None, :] * tau)
    pr, pi = pe * jnp.cos(zi[:, None, :] * tau), pe * jnp.sin(zi[:, None, :] * tau)
    e_r = pr[:, :L, :, None] * bbr[:, None] - pi[:, :L, :, None] * bbi[:, None]
    e_i = pr[:, :L, :, None] * bbi[:, None] + pi[:, :L, :, None] * bbr[:, None]
    taps = (jnp.einsum('gcp,gtpd->gtcd', c_re, e_r, precision=hp)
            - jnp.einsum('gcp,gtpd->gtcd', c_im, e_i, precision=hp))
    s_idx = jnp.arange(L)[:, None]
    t_idx = jnp.arange(L)[None, :]
    lag = jnp.clip(t_idx - s_idx, 0, L - 1)
    toep = jnp.where((t_idx >= s_idx)[None, :, :, None, None], taps[:, lag], 0.0)
    toep = toep.transpose(0, 1, 4, 2, 3).reshape(G, L * CH, L * CH)
    fr, fi = pr[:, L - 1::-1][:, :L], pi[:, L - 1::-1][:, :L]
    b_r = fr[:, :, None, :] * bbr.transpose(0, 2, 1)[:, None] - fi[:, :, None, :] * bbi.transpose(0, 2, 1)[:, None]
    b_i = fr[:, :, None, :] * bbi.transpose(0, 2, 1)[:, None] + fi[:, :, None, :] * bbr.transpose(0, 2, 1)[:, None]
    b_op = jnp.concatenate([b_r, b_i], axis=-1).reshape(G, L * CH, 2 * P)
    gr, gi = pr[:, 1:], pi[:, 1:]
    d_r = c_re[:, None] * gr[:, :, None, :] - c_im[:, None] * gi[:, :, None, :]
    d_i = c_re[:, None] * gi[:, :, None, :] + c_im[:, None] * gr[:, :, None, :]
    c_op = jnp.concatenate([d_r, -d_i], axis=-1).reshape(G, L * CH, 2 * P).transpose(0, 2, 1)
    a1 = jnp.concatenate([pr[:, L], pr[:, L]], axis=-1)[:, None, :]
    a2 = jnp.concatenate([-pi[:, L], pi[:, L]], axis=-1)[:, None, :]
    return toep.astype(BF16), b_op.astype(BF16), c_op.astype(BF16), a1, a2


def _s5_kernel(u_ref, toep_ref, bop_ref, cop_ref, a1_ref, a2_ref, y_ref, bu_ref, xs_ref):
    u = u_ref[0].astype(BF16)
    nc = u.shape[0]
    bu_ref[...] = jnp.dot(u, bop_ref[0], preferred_element_type=F32)
    a1 = a1_ref[0]
    a2 = a2_ref[0]

    def step(n, x):
        xs_ref[pl.ds(n, 1), :] = x
        return a1 * x + a2 * pltpu.roll(x, S5_STATE, axis=1) + bu_ref[pl.ds(n, 1), :]

    lax.fori_loop(0, nc, step, jnp.zeros((1, 2 * S5_STATE), F32))
    y = jnp.dot(u, toep_ref[0], preferred_element_type=F32)
    y = y + jnp.dot(xs_ref[...].astype(BF16), cop_ref[0], preferred_element_type=F32)
    y_ref[0] = y


def s5_scan(u_g, toep, b_op, c_op, a1, a2):
    g, nc, w = u_g.shape
    p2 = 2 * S5_STATE
    blk = lambda *s: pl.BlockSpec((1,) + s, lambda i: (i, 0, 0))
    return pl.pallas_call(
        _s5_kernel, grid=(g,),
        in_specs=[blk(nc, w), blk(w, w), blk(w, p2), blk(p2, w), blk(1, p2), blk(1, p2)],
        out_specs=blk(nc, w), out_shape=jax.ShapeDtypeStruct((g, nc, w), F32),
        scratch_shapes=[pltpu.VMEM((nc, p2), F32), pltpu.VMEM((nc, p2), F32)],
        compiler_params=_cp(("parallel",)), name="s5_scan")(u_g, toep, b_op, c_op, a1, a2)


def _s5_post_kernel(y_ref, u_ref, d_ref, w_ref, bg_ref, ng_ref, o_ref):
    z = jax.nn.gelu(y_ref[...] + d_ref[...] * u_ref[...], approximate=True)
    gate = _sigmoid(jnp.dot(z.astype(BF16), w_ref[...], preferred_element_type=F32) + bg_ref[...])
    o = z * gate
    o = o * lax.rsqrt(jnp.mean(o * o, axis=-1, keepdims=True) + RMS_EPS) * ng_ref[...]
    o_ref[...] = o.astype(o_ref.dtype)


def s5_post(y, proj, d, w_glu, b_glu, norm_g, tb=256):
    t = y.shape[0]
    tb = min(tb, t)
    row = pl.BlockSpec((tb, D_S5), lambda i: (i, 0))
    vec = pl.BlockSpec((1, D_S5), lambda i: (0, 0))
    return pl.pallas_call(
        _s5_post_kernel, grid=(t // tb,),
        in_specs=[row, pl.BlockSpec((tb, D_S5), lambda i: (i, C_S5 // D_S5)), vec,
                  pl.BlockSpec((D_S5, D_S5), lambda i: (0, 0)), vec, vec],
        out_specs=row, out_shape=jax.ShapeDtypeStruct((t, D_S5), BF16),
        compiler_params=_cp(("parallel",)), name="s5_post")(
            y, proj, d.reshape(1, -1), w_glu.astype(BF16), b_glu.reshape(1, -1), norm_g.reshape(1, -1))


SUB = 16


def _gla_chunk(q, k, v, g, st):
    c = q.shape[0]
    b = _cumsum_rows(g)
    b_last = b[c - 1:c, :]
    o_inter = _dot_nt(q * jnp.exp(b), st)
    rows = lax.broadcasted_iota(jnp.int32, (SUB, 1), 0)
    outs = []
    for i0 in range(0, c, SUB):
        bi = b[i0:i0 + SUB]
        qi = q[i0:i0 + SUB]
        ki = k[i0:i0 + SUB]
        vi = v[i0:i0 + SUB]
        acc = o_inter[i0:i0 + SUB]
        if i0 > 0:
            bref = b[i0:i0 + 1]
            qs = qi * jnp.exp(bi - bref)
            ks = k[:i0] * jnp.exp(bref - b[:i0])
            acc = acc + _dot(_dot_nt(qs, ks), v[:i0])
        for j in range(SUB):
            e = jnp.exp(jnp.minimum(bi - bi[j:j + 1], 0.0))
            s = jnp.sum(qi * (ki[j:j + 1] * e), axis=-1, keepdims=True)
            acc = acc + jnp.where(rows >= j, s, 0.0) * vi[j:j + 1]
        outs.append(acc)
    o = jnp.concatenate(outs, axis=0)
    st_new = st * jnp.exp(b_last) + _dot_tn(v, k * jnp.exp(b_last - b))
    return o, st_new


def _gated_rms_out(o, norm_g, gate):
    return o * lax.rsqrt(jnp.mean(o * o, axis=-1, keepdims=True) + RMS_EPS) * norm_g * _silu(gate)


def _hgrn_kernel(q_ref, f_ref, i_ref, g_ref, loglb_ref, log1m_ref, onem_ref, ng_ref, o_ref, st_ref):
    @pl.when(pl.program_id(1) == 0)
    def _():
        st_ref[...] = jnp.zeros_like(st_ref)

    tb = q_ref.shape[0]
    st = st_ref[...]
    for c0 in range(0, tb, CHUNK):
        sl = slice(c0, c0 + CHUNK)
        f = f_ref[sl, :]
        x = log1m_ref[...] + _log_sigmoid(f)
        a = loglb_ref[...]
        log_f = jnp.maximum(a, x) + jnp.log1p(jnp.exp(-jnp.abs(a - x)))
        k = onem_ref[...] * _sigmoid(-f)
        o, st = _gla_chunk(_silu(q_ref[sl, :]), k, i_ref[sl, :], log_f, st)
        o_ref[sl, :] = _gated_rms_out(o, ng_ref[...], g_ref[sl, :]).astype(o_ref.dtype)
    st_ref[...] = st


def hgrn2_mixer(proj, lb, norm_g, tb=128):
    t = proj.shape[0]
    tb = min(tb, t)
    col = lambda c0: pl.BlockSpec((tb, HG_DK), lambda h, i: (i, c0 // HG_DK + h))
    vec = pl.BlockSpec((1, HG_DK), lambda h, i: (0, h))
    lb = lb.reshape(1, -1).astype(F32)
    return pl.pallas_call(
        _hgrn_kernel, grid=(HG_HEADS, t // tb),
        in_specs=[col(C_HGQ), col(C_HGF), col(C_HGI), col(C_HGG), vec, vec, vec, vec],
        out_specs=pl.BlockSpec((tb, HG_DK), lambda h, i: (i, h)),
        out_shape=jax.ShapeDtypeStruct((t, D_HG), BF16),
        scratch_shapes=[pltpu.VMEM((HG_DK, HG_DK), F32)],
        compiler_params=_cp(("parallel", "arbitrary")), name="hgrn2")(
            proj, proj, proj, proj, jnp.log(lb), jnp.log1p(-lb), 1.0 - lb, norm_g.reshape(1, -1))


def _gla_kernel(q_ref, k_ref, v_ref, g_ref, a_ref, wa_ref, ba_ref, ng_ref, o_ref, st_ref):
    @pl.when(pl.program_id(1) == 0)
    def _():
        st_ref[...] = jnp.zeros_like(st_ref)

    tb = q_ref.shape[0]
    st = st_ref[...]
    for c0 in range(0, tb, CHUNK):
        sl = slice(c0, c0 + CHUNK)
        logit = jnp.dot(a_ref[sl, :].astype(BF16), wa_ref[...], preferred_element_type=F32) + ba_ref[...]
        log_alpha = _log_sigmoid(logit) * (1.0 / GLA_TAU)
        o, st = _gla_chunk(q_ref[sl, :] * GLA_DK ** -0.5, k_ref[sl, :], v_ref[sl, :], log_alpha, st)
        o_ref[sl, :] = _gated_rms_out(o, ng_ref[...], g_ref[sl, :]).astype(o_ref.dtype)
    st_ref[...] = st


def gla_mixer(proj, w_a2, b_a, norm_g, tb=128):
    t = proj.shape[0]
    tb = min(tb, t)
    wa = jnp.zeros((LANE, GLA_QK), F32).at[:GLA_LORA].set(w_a2).astype(BF16)
    kcol = lambda c0: pl.BlockSpec((tb, GLA_DK), lambda h, i: (i, c0 // GLA_DK + h))
    vcol = lambda c0: pl.BlockSpec((tb, GLA_DV), lambda h, i: (i, c0 // GLA_DV + h))
    return pl.pallas_call(
        _gla_kernel, grid=(GLA_HEADS, t // tb),
        in_specs=[kcol(C_GQ), kcol(C_GK), vcol(C_GV), vcol(C_GG),
                  pl.BlockSpec((tb, LANE), lambda h, i: (i, C_GA // LANE)),
                  pl.BlockSpec((LANE, GLA_DK), lambda h, i: (0, h)),
                  pl.BlockSpec((1, GLA_DK), lambda h, i: (0, h)),
                  pl.BlockSpec((1, GLA_DV), lambda h, i: (0, h))],
        out_specs=pl.BlockSpec((tb, GLA_DV), lambda h, i: (i, h)),
        out_shape=jax.ShapeDtypeStruct((t, D_GLA), BF16),
        scratch_shapes=[pltpu.VMEM((GLA_DV, GLA_DK), F32)],
        compiler_params=_cp(("parallel", "arbitrary")), name="gla")(
            proj, proj, proj, proj, proj, wa, b_a.reshape(1, -1), norm_g.reshape(1, -1))


def _shift_mix(cur_ref, prev_ref, mu_ref):
    p = cur_ref[...]
    first = jnp.where(pl.program_id(0) == 0, 0.0, 1.0) * prev_ref[7:8, :]
    rows = lax.broadcasted_iota(jnp.int32, p.shape, 0)
    shifted = jnp.where(rows == 0, first, pltpu.roll(p, 1, axis=0))
    return p + mu_ref[...] * (shifted - p)


def _rw_prep_kernel(has_vres, *refs):
    (r_ref, rp_ref, k_ref, kp_ref, v_ref, vp_ref, xwa_ref, xwap_ref, xg_ref, xgp_ref,
     mu_r, mu_k, mu_v, mu_xwa, mu_xg, w0_ref, w2_ref, a0_ref, a2_ref, g2_ref,
     kk_ref, ka_ref) = refs[:22]
    refs = refs[22:]
    if has_vres:
        vf_ref, v0_ref, v1_ref, v2_ref = refs[:4]
        refs = refs[4:]
    ro_ref, lw_ref, ko_ref, vo_ref, ao_ref, bo_ref, gate_ref = refs[:7]
    r = _shift_mix(r_ref, rp_ref, mu_r)
    k = _shift_mix(k_ref, kp_ref, mu_k)
    v = _shift_mix(v_ref, vp_ref, mu_v)
    xwa = _shift_mix(xwa_ref, xwap_ref, mu_xwa)
    xg = _shift_mix(xg_ref, xgp_ref, mu_xg)
    if has_vres:
        mix = _sigmoid(v0_ref[...] + _dot(_dot(v, v1_ref[...]), v2_ref[...]))
        v = v + (vf_ref[...] - v) * mix
    else:
        refs[7][...] = v
    z = w0_ref[...] + _dot(jnp.tanh(xwa), w2_ref[...])
    w_log = _log_sigmoid(z) - 0.5
    lw = -jnp.exp(w_log)
    alr = _sigmoid(a0_ref[...] + _dot(xwa, a2_ref[...]))
    gate_ref[...] = _dot(_sigmoid(xg), g2_ref[...])
    kk_all = k * kk_ref[...]
    k2 = k * (1.0 + (alr - 1.0) * ka_ref[...])
    for h in range(RW_HEADS):
        sl = slice(h * RW_HEAD, (h + 1) * RW_HEAD)
        kk = kk_all[:, sl]
        nrm = jnp.sqrt(jnp.sum(kk * kk, axis=-1, keepdims=True))
        kk = kk / jnp.maximum(nrm, 1e-12)
        ro_ref[h] = r[:, sl]
        lw_ref[h] = lw[:, sl]
        ko_ref[h] = k2[:, sl]
        vo_ref[h] = v[:, sl]
        ao_ref[h] = -kk
        bo_ref[h] = kk * alr[:, sl]


def rwkv_prep(proj, mu_r, mu_k, mu_v, mu_xwa, mu_xg, w0, w2p, a0, a2p, g2p, k_k, k_a, vres, tb=128):
    t = proj.shape[0]
    tb = min(tb, t)
    c = D_RW

    def cur(c0, w):
        return pl.BlockSpec((tb, w), lambda i: (i, c0 // w))

    def prev(c0, w):
        return pl.BlockSpec((8, w), lambda i: (jnp.maximum(i * (tb // 8) - 1, 0), c0 // w))

    full = lambda a: pl.BlockSpec(a.shape, lambda i: (0,) * a.ndim)
    vec = lambda a: a.reshape(1, -1)
    ins = [proj] * 10
    specs = [cur(C_RR, c), prev(C_RR, c), cur(C_RK, c), prev(C_RK, c), cur(C_RV, c), prev(C_RV, c),
             cur(C_XWA, LANE), prev(C_XWA, LANE), cur(C_XG, 2 * LANE), prev(C_XG, 2 * LANE)]
    smalls = [vec(mu_r), vec(mu_k), vec(mu_v), vec(mu_xwa), vec(mu_xg), vec(w0), w2p, vec(a0), a2p, g2p,
              vec(k_k), vec(k_a)]
    ins += smalls
    specs += [full(a) for a in smalls]
    tok = pl.BlockSpec((tb, c), lambda i: (i, 0))
    if vres is not None:
        v_first, v0, v1p, v2p = vres
        extra = [vec(v0), v1p, v2p]
        ins += [v_first] + extra
        specs += [tok] + [full(a) for a in extra]
    hm = pl.BlockSpec((RW_HEADS, tb, RW_HEAD), lambda i: (0, i, 0))
    hm_shape = jax.ShapeDtypeStruct((RW_HEADS, t, RW_HEAD), F32)
    out_specs = [hm] * 6 + [tok]
    out_shape = [hm_shape] * 6 + [jax.ShapeDtypeStruct((t, c), F32)]
    if vres is None:
        out_specs.append(tok)
        out_shape.append(jax.ShapeDtypeStruct((t, c), F32))
    return pl.pallas_call(
        functools.partial(_rw_prep_kernel, vres is not None), grid=(t // tb,),
        in_specs=specs, out_specs=out_specs, out_shape=out_shape,
        compiler_params=_cp(("parallel",)), name="rwkv_prep")(*ins)


def _rw_chunk(r, k, v, a, b, lw, z):
    c, n = r.shape
    cum = _cumsum_rows(lw)
    cum_last = cum[c - 1:c, :]
    at = a * jnp.exp(cum - lw)
    en = jnp.exp(-cum)
    bh, kh = b * en, k * en
    rt = r * jnp.exp(cum)
    el = jnp.exp(cum_last - cum)
    bb, kb = b * el, k * el
    lhs = jnp.concatenate([at, rt], axis=0)
    ab = _dot_nt(lhs, bh)
    ak = _dot_nt(lhs, kh)
    strict = _tril(c, strict=True)
    incl = _tril(c)
    a_ab = jnp.where(strict, ab[:c], 0.0)
    a_ak = jnp.where(strict, ak[:c], 0.0)
    a_rb = jnp.where(incl, ab[c:], 0.0)
    a_rk = jnp.where(incl, ak[c:], 0.0)
    x = jnp.concatenate([at, _dot(a_ak, v)], axis=1)
    nil = a_ab
    steps = int(math.log2(c))
    for s in range(steps):
        x = x + _dot(nil, x)
        if s + 1 < steps:
            nil = _dot(nil, nil)
    rx = _dot(a_rb, x)
    bx = _dot_tn(bb, x)
    eye = lax.broadcasted_iota(jnp.int32, (n, n), 0) == lax.broadcasted_iota(jnp.int32, (n, n), 1)
    m = jnp.where(eye, jnp.exp(cum_last), 0.0) + bx[:, :n]
    nn = bx[:, n:] + _dot_tn(kb, v)
    y = _dot(rt + rx[:, :n], z) + rx[:, n:] + _dot(a_rk, v)
    z_new = _dot(m, z) + nn
    return y, z_new


def _rw_scan_kernel(r_ref, lw_ref, k_ref, v_ref, a_ref, b_ref, gate_ref, rk_ref, lng_ref, lnb_ref,
                    o_ref, z_ref):
    @pl.when(pl.program_id(1) == 0)
    def _():
        z_ref[...] = jnp.zeros_like(z_ref)

    tb = r_ref.shape[1]
    zs = [z_ref[0], z_ref[1]]
    for c0 in range(0, tb, CHUNK):
        sl = slice(c0, c0 + CHUNK)
        outs = []
        for hh in range(2):
            r, k, v = r_ref[hh, sl, :], k_ref[hh, sl, :], v_ref[hh, sl, :]
            y, zs[hh] = _rw_chunk(r, k, v, a_ref[hh, sl, :], b_ref[hh, sl, :], lw_ref[hh, sl, :], zs[hh])
            mu = jnp.mean(y, axis=-1, keepdims=True)
            yc = y - mu
            var = jnp.mean(yc * yc, axis=-1, keepdims=True)
            yn = yc * lax.rsqrt(var + RW_GN_EPS) * lng_ref[hh] + lnb_ref[hh]
            bonus = jnp.sum(r * k * rk_ref[hh], axis=-1, keepdims=True) * v
            outs.append(yn + bonus)
        o_ref[sl, :] = (jnp.concatenate(outs, axis=1) * gate_ref[sl, :]).astype(o_ref.dtype)
    z_ref[0] = zs[0]
    z_ref[1] = zs[1]


def rwkv_scan(r, lw, k, v, a, b, gate, r_k, ln_g, ln_b, tb=128):
    t = r.shape[1]
    tb = min(tb, t)
    hm = pl.BlockSpec((2, tb, RW_HEAD), lambda p, i: (p, i, 0))
    par = pl.BlockSpec((2, 1, RW_HEAD), lambda p, i: (p, 0, 0))
    hp = lambda x: x.reshape(RW_HEADS, 1, RW_HEAD)
    return pl.pallas_call(
        _rw_scan_kernel, grid=(RW_HEADS // 2, t // tb),
        in_specs=[hm] * 6 + [pl.BlockSpec((tb, LANE), lambda p, i: (i, p)), par, par, par],
        out_specs=pl.BlockSpec((tb, LANE), lambda p, i: (i, p)),
        out_shape=jax.ShapeDtypeStruct((t, D_RW), BF16),
        scratch_shapes=[pltpu.VMEM((2, RW_HEAD, RW_HEAD), F32)],
        compiler_params=_cp(("parallel", "arbitrary")), name="rwkv_scan")(
            r, lw, k, v, a, b, gate, hp(r_k), hp(ln_g), hp(ln_b))


def _first_max(x, ids, big):
    m = jnp.max(x, axis=0, keepdims=True)
    return m, jnp.min(jnp.where(x == m, ids, big), axis=0, keepdims=True)


def _router_kernel(h_ref, wr_ref, bias_ref, o_ref):
    h = h_ref[...]
    h_hi = h.astype(BF16)
    h_lo = (h - h_hi.astype(F32)).astype(BF16)
    w = wr_ref[...]
    w_hi = w.astype(BF16)
    w_lo = (w - w_hi.astype(F32)).astype(BF16)
    nt = lambda a, b: lax.dot_general(a, b, (((1,), (1,)), ((), ())), preferred_element_type=F32)
    logits = nt(w_hi, h_hi) + nt(w_hi, h_lo) + nt(w_lo, h_hi)
    scores = _sigmoid(logits)
    sel = scores + bias_ref[...]
    tb = h.shape[0]
    per = N_EXPERTS // N_GROUPS
    sub = lax.broadcasted_iota(jnp.int32, (per, tb), 0)
    neg = -jnp.inf
    gscore = jnp.zeros((N_GROUPS, tb), F32)
    gid = lax.broadcasted_iota(jnp.int32, (N_GROUPS, tb), 0)
    for g in range(N_GROUPS):
        x = sel[g * per:(g + 1) * per]
        m1, i1 = _first_max(x, sub, per)
        m2 = jnp.max(jnp.where(sub == i1, neg, x), axis=0, keepdims=True)
        gscore = jnp.where(gid == g, m1 + m2, gscore)
    gmask = jnp.zeros((N_GROUPS, tb), F32)
    for _ in range(TOPK_GROUPS):
        _, gi = _first_max(gscore, gid, N_GROUPS)
        hit = gid == gi
        gmask = jnp.where(hit, 1.0, gmask)
        gscore = jnp.where(hit, neg, gscore)
    cand = [jnp.where(gmask[g:g + 1] > 0.0, sel[g * per:(g + 1) * per], neg) for g in range(N_GROUPS)]
    ids = [sub + g * per for g in range(N_GROUPS)]
    chosen = [jnp.zeros((per, tb), F32) for _ in range(N_GROUPS)]
    for _ in range(TOP_K):
        best = [_first_max(cand[g], ids[g], N_EXPERTS) for g in range(N_GROUPS)]
        m = functools.reduce(jnp.maximum, [bm for bm, _ in best])
        idx = functools.reduce(jnp.minimum, [jnp.where(bm == m, bi, N_EXPERTS) for bm, bi in best])
        for g in range(N_GROUPS):
            hit = ids[g] == idx
            chosen[g] = jnp.where(hit, 1.0, chosen[g])
            cand[g] = jnp.where(hit, neg, cand[g])
    wts = [chosen[g] * scores[g * per:(g + 1) * per] for g in range(N_GROUPS)]
    tot = functools.reduce(lambda p, q: p + q, [jnp.sum(wg, axis=0, keepdims=True) for wg in wts])
    for g in range(N_GROUPS):
        o_ref[g * per:(g + 1) * per, :] = wts[g] / tot * ROUTED_SCALE
    tail = lax.broadcasted_iota(jnp.int32, (LANE - N_EXPERTS, tb), 0)
    o_ref[N_EXPERTS:, :] = jnp.where(tail < D_SHARED // D_EXPERT, 1.0, 0.0)


def route_tokens(h32, router, bias, tb=512):
    t, d = h32.shape
    tb = min(tb, t)
    out = pl.pallas_call(
        _router_kernel, grid=(t // tb,),
        in_specs=[pl.BlockSpec((tb, d), lambda i: (i, 0)), pl.BlockSpec((N_EXPERTS, d), lambda i: (0, 0)),
                  pl.BlockSpec((N_EXPERTS, 1), lambda i: (0, 0))],
        out_specs=pl.BlockSpec((LANE, tb), lambda i: (0, i)),
        out_shape=jax.ShapeDtypeStruct((LANE, t), F32),
        compiler_params=_cp(("parallel",)), name="moe_router")(h32, router.T, bias.reshape(-1, 1))
    return out.T


def _moe_up_kernel(x_ref, wg_ref, wu_ref, c_ref, e_ref, o_ref):
    x = x_ref[...]
    g = jnp.dot(x, wg_ref[...], preferred_element_type=F32)
    u = jnp.dot(x, wu_ref[...], preferred_element_type=F32)
    c = c_ref[...]
    c_hi = c.astype(BF16)
    c_lo = (c - c_hi.astype(F32)).astype(BF16)
    e = e_ref[...]
    s = jnp.dot(c_hi, e, preferred_element_type=F32) + jnp.dot(c_lo, e, preferred_element_type=F32)
    o_ref[...] = (_silu(g) * u * s).astype(o_ref.dtype)


def moe_up(hb, wg, wu, comb, expand, tm=512, tn=512):
    t, d = hb.shape
    tm = min(tm, t)
    return pl.pallas_call(
        _moe_up_kernel, grid=(t // tm, MOE_HID // tn),
        in_specs=[pl.BlockSpec((tm, d), lambda i, j: (i, 0)), pl.BlockSpec((d, tn), lambda i, j: (0, j)),
                  pl.BlockSpec((d, tn), lambda i, j: (0, j)), pl.BlockSpec((tm, LANE), lambda i, j: (i, 0)),
                  pl.BlockSpec((LANE, tn), lambda i, j: (0, j))],
        out_specs=pl.BlockSpec((tm, tn), lambda i, j: (i, j)),
        out_shape=jax.ShapeDtypeStruct((t, MOE_HID), BF16),
        compiler_params=_cp(("parallel", "parallel")), name="moe_up")(hb, wg, wu, comb, expand)


def _pad_rows(w, rows):
    return jnp.zeros((rows, w.shape[1]), w.dtype).at[:w.shape[0]].set(w)


def _pad_cols(w, cols):
    return jnp.zeros((w.shape[0], cols), w.dtype).at[:, :w.shape[1]].set(w)


def pack_w_in(w):
    b = REF_BASE_COLS
    rkv = w[:, b:b + 3 * D_RW]
    xwa = w[:, b + 3 * D_RW:b + 3 * D_RW + RW_LORA_W + RW_LORA_A]
    xg = w[:, b + 3 * D_RW + RW_LORA_W + RW_LORA_A:]
    return jnp.concatenate([w[:, :C_RR], rkv, _pad_cols(w[:, C_RR:b], LANE), xwa, _pad_cols(xg, 2 * LANE)],
                           axis=1).astype(BF16)


def expand_matrix():
    e = lax.broadcasted_iota(jnp.int32, (LANE, MOE_HID), 0)
    c = lax.broadcasted_iota(jnp.int32, (LANE, MOE_HID), 1)
    return jnp.where(e == c // D_EXPERT, 1.0, 0.0).astype(BF16)


def _s5_layer(proj, ops, d, w_glu, b_glu, norm_g):
    t = proj.shape[0]
    nc = t // CHUNK
    u = proj[:, C_S5:C_S5 + D_S5].reshape(nc, CHUNK, S5_GROUPS, S5_CH)
    u_g = u.transpose(2, 0, 1, 3).reshape(S5_GROUPS, nc, CHUNK * S5_CH)
    y_g = s5_scan(u_g, *ops)
    y = y_g.reshape(S5_GROUPS, nc, CHUNK, S5_CH).transpose(1, 2, 0, 3).reshape(t, D_S5)
    return s5_post(y, proj, d, w_glu, b_glu, norm_g)


def kernel(x, ln_in_g, ln_in_b, w_in, s5_a_re, s5_a_im, s5_log_dt, s5_b_re, s5_b_im, s5_c_re, s5_c_im, s5_d, s5_w_glu, s5_b_glu, s5_norm_g, hg_lb, hg_norm_g, gla_w_a2, gla_b_a, gla_norm_g, rw_mu, rw_w0, rw_w2, rw_a0, rw_a2, rw_g2, rw_v0, rw_v1, rw_v2, rw_k_k, rw_k_a, rw_r_k, rw_ln_g, rw_ln_b, w_out, ln1_g, ln1_b, moe_router, moe_bias, moe_w_gate, moe_w_up, moe_w_down, sh_w_gate, sh_w_up, sh_w_down, ln2_g, ln2_b):
    bsz, t, d = x.shape
    assert bsz == 1 and d == D_MODEL and t % CHUNK == 0
    lb_all = jnp.cumsum(jax.nn.softmax(hg_lb.astype(F32), axis=0), axis=0)
    lb_all = lb_all - lb_all[0]
    expand = expand_matrix()
    h32, h16 = layer_norm(x.reshape(t, d), ln_in_g, ln_in_b)
    v_first = None
    for l in range(DEPTH):
        proj = matmul(h16, pack_w_in(w_in[l]), 1024, 512, F32)

        ops = s5_operators(s5_a_re[l], s5_a_im[l], s5_log_dt[l], s5_b_re[l], s5_b_im[l], s5_c_re[l], s5_c_im[l])
        y_s5 = _s5_layer(proj, ops, s5_d[l], s5_w_glu[l], s5_b_glu[l], s5_norm_g[l])
        y_hg = hgrn2_mixer(proj, lb_all[l], hg_norm_g[l])
        y_gla = gla_mixer(proj, gla_w_a2[l], gla_b_a[l], gla_norm_g[l])

        mu = rw_mu[l]
        mu_xg = jnp.zeros((2 * LANE,), F32).at[:RW_LORA_G].set(mu[3 * D_RW + RW_LORA_W + RW_LORA_A:])
        w2p = _pad_rows(rw_w2[l], LANE).astype(BF16)
        a2p = jnp.zeros((LANE, D_RW), F32).at[RW_LORA_W:].set(rw_a2[l]).astype(BF16)
        g2p = _pad_rows(rw_g2[l], 2 * LANE).astype(BF16)
        vres = None
        if l > 0:
            vres = (v_first, rw_v0[l - 1], _pad_cols(rw_v1[l - 1], LANE).astype(BF16),
                    _pad_rows(rw_v2[l - 1], LANE).astype(BF16))
        outs = rwkv_prep(proj, mu[:D_RW], mu[D_RW:2 * D_RW], mu[2 * D_RW:3 * D_RW],
                         mu[3 * D_RW:3 * D_RW + RW_LORA_W + RW_LORA_A], mu_xg, rw_w0[l], w2p, rw_a0[l], a2p,
                         g2p, rw_k_k[l], rw_k_a[l], vres)
        if l == 0:
            v_first = outs[7]
        y_rw = rwkv_scan(*outs[:7], rw_r_k[l], rw_ln_g[l], rw_ln_b[l])

        cat = jnp.concatenate([y_s5, y_hg, y_gla, y_rw], axis=-1)
        mix = matmul(cat, w_out[l].astype(BF16), 1024, 512, F32)
        h32, h16 = residual_layer_norm(h32, mix, ln1_g[l], ln1_b[l])

        comb = route_tokens(h32, moe_router[l], moe_bias[l])
        wg = jnp.concatenate([moe_w_gate[l].transpose(1, 0, 2).reshape(d, -1), sh_w_gate[l]], axis=1).astype(BF16)
        wu = jnp.concatenate([moe_w_up[l].transpose(1, 0, 2).reshape(d, -1), sh_w_up[l]], axis=1).astype(BF16)
        wd = jnp.concatenate([moe_w_down[l].reshape(-1, d), sh_w_down[l]], axis=0).astype(BF16)
        hid = moe_up(h16, wg, wu, comb, expand)
        ffn = matmul(hid, wd, 512, 512, F32)
        h32, h16 = residual_layer_norm(h32, ffn, ln2_g[l], ln2_b[l])
    return h32.reshape(bsz, t, d).astype(x.dtype)
```
